```python
import jax, jax.numpy as jnp
from jax import lax
import numpy as np

D_MODEL = 1024
BATCH = 2
SEQ = 8192
DEPTH = 1
DEC_BATCH = 32
DEC_SEQ = 4
PAST_LEN = 16384
PAGE_SIZE = 128

HEAD_DIM = 64
N_HEADS_MOBA = 8
N_HEADS_SB = 8
W_MOBA = N_HEADS_MOBA * HEAD_DIM
W_SB = N_HEADS_SB * HEAD_DIM
D_IN = 3 * W_MOBA + 3 * W_SB + 2 * D_MODEL
MOBA_BLOCK = 256
MOBA_TOPK = 3
MOBA_Q_CHUNK = 32
SB_Q_BLOCK = 128
ROPE_THETA = 10000.0
N_GROUPS = 4
EXPERTS_PER_GROUP = 4
N_EXPERTS = N_GROUPS * EXPERTS_PER_GROUP
TOPK_IN_GROUP = 2
D_EXPERT = 512
RMS_EPS = 1e-6

kernel_name = "moba_stickbreaking_hmoe_decode_step"


def rms_norm(x, g):
    xf = x.astype(jnp.float32)
    y = xf * lax.rsqrt(jnp.mean(xf * xf, axis=-1, keepdims=True) + RMS_EPS)
    return (y * g.astype(jnp.float32)).astype(x.dtype)


def rope(x, pos):
    half = HEAD_DIM // 2
    inv_freq = ROPE_THETA ** (-jnp.arange(half, dtype=jnp.float32) / half)
    ang = pos.astype(jnp.float32)[:, None] * inv_freq[None, :]
    cos = jnp.cos(ang)[None, :, None, :]
    sin = jnp.sin(ang)[None, :, None, :]
    xf = x.astype(jnp.float32)
    x1, x2 = xf[..., :half], xf[..., half:]
    return jnp.concatenate([x1 * cos - x2 * sin, x2 * cos + x1 * sin], axis=-1).astype(x.dtype)


def moba_attention(q, k, v, q_pos):
    B, Tq, H, dh = q.shape
    Lk = k.shape[1]
    nb = -(-Lk // MOBA_BLOCK)
    pad = nb * MOBA_BLOCK - Lk
    kp = jnp.pad(k, ((0, 0), (0, pad), (0, 0), (0, 0))).reshape(B, nb, MOBA_BLOCK, H, dh)
    vp = jnp.pad(v, ((0, 0), (0, pad), (0, 0), (0, 0))).reshape(B, nb, MOBA_BLOCK, H, dh)
    k_mean = jnp.mean(kp.astype(jnp.float32), axis=2)
    n_sel = min(MOBA_TOPK, nb)
    bi = jnp.arange(B)[:, None, None, None]
    hi = jnp.arange(H)[None, :, None, None]
    offs = jnp.arange(MOBA_BLOCK, dtype=jnp.int32)
    scale = HEAD_DIM ** -0.5

    def chunk(args):
        qc, pc = args
        C = qc.shape[1]
        qf = qc.astype(jnp.float32)
        own = pc // MOBA_BLOCK
        gate = jnp.einsum('bchd,bnhd->bhcn', qf, k_mean)
        fully_past = jnp.arange(nb)[None, :] < own[:, None]
        gate = jnp.where(fully_past, gate, -jnp.inf)
        _, sel = lax.top_k(gate, n_sel)
        sel_valid = sel < own[:, None]
        blocks = jnp.concatenate([sel, jnp.broadcast_to(own[:, None], (B, H, C, 1))], axis=-1)
        valid = jnp.concatenate([sel_valid, jnp.ones((B, H, C, 1), dtype=bool)], axis=-1)
        kg = kp[bi, blocks, :, hi]
        vg = vp[bi, blocks, :, hi]
        kpos = blocks[..., None] * MOBA_BLOCK + offs
        mask = valid[..., None] & (kpos <= pc[:, None, None])
        s = jnp.einsum('bchd,bhcsld->bhcsl', qf, kg.astype(jnp.float32)) * scale
        s = jnp.where(mask, s, -jnp.inf)
        S = s.shape[3]
        p = jax.nn.softmax(s.reshape(B, H, C, S * MOBA_BLOCK), axis=-1).reshape(B, H, C, S, MOBA_BLOCK)
        out = jnp.einsum('bhcsl,bhcsld->bchd', p, vg.astype(jnp.float32))
        return out.astype(q.dtype)

    C = MOBA_Q_CHUNK if Tq % MOBA_Q_CHUNK == 0 else Tq
    nc = Tq // C
    qs = q.reshape(B, nc, C, H, dh).transpose(1, 0, 2, 3, 4)
    ps = q_pos.reshape(nc, C)
    out = lax.map(chunk, (qs, ps))
    return out.transpose(1, 0, 2, 3, 4).reshape(B, Tq, H, dh)


def stick_breaking_attention(q, k, v, q_pos):
    B, Tq, H, dh = q.shape
    Lk = k.shape[1]
    kf = k.astype(jnp.float32)
    vf = v.astype(jnp.float32)
    kpos = jnp.arange(Lk, dtype=jnp.int32)
    scale = HEAD_DIM ** -0.5

    def block(args):
        qc, pc = args
        z = jnp.einsum('bchd,blhd->bhcl', qc.astype(jnp.float32), kf) * scale
        m = kpos[None, :] < pc[:, None]
        log_1mb = jnp.where(m, jax.nn.log_sigmoid(-z), 0.0)
        log_rest = lax.cumsum(log_1mb, axis=3, reverse=True) - log_1mb
        a = jnp.where(m, jnp.exp(jax.nn.log_sigmoid(z) + log_rest), 0.0)
        return jnp.einsum('bhcl,blhd->bchd', a, vf).astype(q.dtype)

    C = SB_Q_BLOCK if Tq % SB_Q_BLOCK == 0 else Tq
    nc = Tq // C
    qs = q.reshape(B, nc, C, H, dh).transpose(1, 0, 2, 3, 4)
    ps = q_pos.reshape(nc, C)
    out = lax.map(block, (qs, ps))
    return out.transpose(1, 0, 2, 3, 4).reshape(B, Tq, H, dh)


def hierarchical_moe(h, w_grp_router, b_grp_router, w_exp_router, b_exp_router, w_gate, w_up, w_down):
    hf = h.astype(jnp.float32)
    grp_logits = hf @ w_grp_router.astype(jnp.float32) + b_grp_router.astype(jnp.float32)
    grp_prob = jax.nn.softmax(grp_logits, axis=-1)
    grp_p, grp_idx = lax.top_k(grp_prob, 1)
    exp_logits = (hf @ w_exp_router.astype(jnp.float32) + b_exp_router.astype(jnp.float32))
    exp_logits = exp_logits.reshape(h.shape[0], N_GROUPS, EXPERTS_PER_GROUP)
    grp_onehot = jax.nn.one_hot(grp_idx[:, 0], N_GROUPS, dtype=jnp.float32)
    in_logits = jnp.einsum('nge,ng->ne', exp_logits, grp_onehot)
    in_prob = jax.nn.softmax(in_logits, axis=-1)
    top_p, top_e = lax.top_k(in_prob, TOPK_IN_GROUP)
    top_p = top_p / jnp.sum(top_p, axis=-1, keepdims=True)
    expert_id = grp_idx * EXPERTS_PER_GROUP + top_e
    weight = grp_p * top_p
    gates = jnp.einsum('nk,nke->ne', weight, jax.nn.one_hot(expert_id, N_EXPERTS, dtype=jnp.float32))
    a = jnp.einsum('nd,edf->nef', h, w_gate)
    u = jnp.einsum('nd,edf->nef', h, w_up)
    hid = jax.nn.silu(a) * u * gates[:, :, None].astype(h.dtype)
    return jnp.einsum('nef,efd->nd', hid, w_down)


def decoder_layer(x, pos, past, ln1_g, w_in, q_norm_g, k_norm_g, w_branch_moba, w_branch_sb,
                  w_out, ln2_g, w_grp_router, b_grp_router, w_exp_router, b_exp_router,
                  w_gate, w_up, w_down):
    B, T, _ = x.shape
    h = rms_norm(x, ln1_g)
    proj = h @ w_in
    sizes = (W_MOBA, W_MOBA, W_MOBA, W_SB, W_SB, W_SB, D_MODEL)
    splits = [int(s) for s in np.cumsum(sizes)]
    q_a, k_a, v_a, q_b, k_b, v_b, g_a, g_b = jnp.split(proj, splits, axis=-1)
    q_a = rope(rms_norm(q_a.reshape(B, T, N_HEADS_MOBA, HEAD_DIM), q_norm_g), pos)
    k_a = rope(rms_norm(k_a.reshape(B, T, N_HEADS_MOBA, HEAD_DIM), k_norm_g), pos)
    v_a = v_a.reshape(B, T, N_HEADS_MOBA, HEAD_DIM)
    q_b = q_b.reshape(B, T, N_HEADS_SB, HEAD_DIM)
    k_b = k_b.reshape(B, T, N_HEADS_SB, HEAD_DIM)
    v_b = v_b.reshape(B, T, N_HEADS_SB, HEAD_DIM)
    if past is None:
        ka_all, va_all, kb_all, vb_all = k_a, v_a, k_b, v_b
    else:
        ka_all = jnp.concatenate([past[0].astype(k_a.dtype), k_a], axis=1)
        va_all = jnp.concatenate([past[1].astype(v_a.dtype), v_a], axis=1)
        kb_all = jnp.concatenate([past[2].astype(k_b.dtype), k_b], axis=1)
        vb_all = jnp.concatenate([past[3].astype(v_b.dtype), v_b], axis=1)
    o_a = moba_attention(q_a, ka_all, va_all, pos).reshape(B, T, W_MOBA)
    o_b = stick_breaking_attention(q_b, kb_all, vb_all, pos).reshape(B, T, W_SB)
    y_a = o_a @ w_branch_moba
    y_b = o_b @ w_branch_sb
    merged = jax.nn.sigmoid(g_a) * y_a + jax.nn.sigmoid(g_b) * y_b
    x = x + merged @ w_out
    h2 = rms_norm(x, ln2_g).reshape(B * T, D_MODEL)
    x = x + hierarchical_moe(h2, w_grp_router, b_grp_router, w_exp_router, b_exp_router,
                             w_gate, w_up, w_down).reshape(B, T, D_MODEL)
    return x, (k_a, v_a, k_b, v_b)


def gather_pages(cache, page_table):
    rows = cache[page_table]
    db, n_pages, ps, h, dh = rows.shape
    return rows.reshape(db, n_pages * ps, h, dh)


def setup_inputs(seed: int = 0) -> dict:
    key = jax.random.key(seed)
    ks = jax.random.split(key, 24)
    n_pages = PAST_LEN // PAGE_SIZE
    used = DEC_BATCH * n_pages
    n_phys = used + max(1, used // 4)
    f32 = jnp.float32
    nrm = lambda k, shape, s: jax.random.normal(k, shape, f32) * s
    page_table = jax.random.permutation(ks[6], n_phys)[:used].reshape(DEC_BATCH, n_pages).astype(jnp.int32)
    return {
        "x_prompt": nrm(ks[0], (BATCH, SEQ, D_MODEL), 1.0),
        "x_sample": nrm(ks[1], (DEC_BATCH, DEC_SEQ, D_MODEL), 1.0),
        "cache_k_moba": nrm(ks[2], (DEPTH, n_phys, PAGE_SIZE, N_HEADS_MOBA, HEAD_DIM), 1.0),
        "cache_v_moba": nrm(ks[3], (DEPTH, n_phys, PAGE_SIZE, N_HEADS_MOBA, HEAD_DIM), 1.0),
        "cache_k_sb": nrm(ks[4], (DEPTH, n_phys, PAGE_SIZE, N_HEADS_SB, HEAD_DIM), 1.0),
        "cache_v_sb": nrm(ks[5], (DEPTH, n_phys, PAGE_SIZE, N_HEADS_SB, HEAD_DIM), 1.0),
        "page_table": page_table,
        "ln1_g": 1.0 + nrm(ks[7], (DEPTH, D_MODEL), 0.02),
        "w_in": nrm(ks[8], (DEPTH, D_MODEL, D_IN), D_MODEL ** -0.5),
        "q_norm_g": 1.0 + nrm(ks[9], (DEPTH, HEAD_DIM), 0.02),
        "k_norm_g": 1.0 + nrm(ks[10], (DEPTH, HEAD_DIM), 0.02),
        "w_branch_moba": nrm(ks[11], (DEPTH, W_MOBA, D_MODEL), W_MOBA ** -0.5),
        "w_branch_sb": nrm(ks[12], (DEPTH, W_SB, D_MODEL), W_SB ** -0.5),
        "w_out": nrm(ks[13], (DEPTH, D_MODEL, D_MODEL), D_MODEL ** -0.5),
        "ln2_g": 1.0 + nrm(ks[14], (DEPTH, D_MODEL), 0.02),
        "w_grp_router": nrm(ks[15], (DEPTH, D_MODEL, N_GROUPS), D_MODEL ** -0.5),
        "b_grp_router": nrm(ks[16], (DEPTH, N_GROUPS), 0.01),
        "w_exp_router": nrm(ks[17], (DEPTH, D_MODEL, N_EXPERTS), D_MODEL ** -0.5),
        "b_exp_router": nrm(ks[18], (DEPTH, N_EXPERTS), 0.01),
        "w_gate": nrm(ks[19], (DEPTH, N_EXPERTS, D_MODEL, D_EXPERT), D_MODEL ** -0.5),
        "w_up": nrm(ks[20], (DEPTH, N_EXPERTS, D_MODEL, D_EXPERT), D_MODEL ** -0.5),
        "w_down": nrm(ks[21], (DEPTH, N_EXPERTS, D_EXPERT, D_MODEL), D_EXPERT ** -0.5),
    }


def _stack(rows, i):
    return jnp.stack([r[i] for r in rows], axis=0)


def reference(x_prompt, x_sample, cache_k_moba, cache_v_moba, cache_k_sb, cache_v_sb, page_table,
              ln1_g, w_in, q_norm_g, k_norm_g, w_branch_moba, w_branch_sb, w_out, ln2_g,
              w_grp_router, b_grp_router, w_exp_router, b_exp_router, w_gate, w_up, w_down):
    past_len = page_table.shape[1] * cache_k_moba.shape[2]
    pos_p = jnp.arange(x_prompt.shape[1], dtype=jnp.int32)
    pos_s = past_len + jnp.arange(x_sample.shape[1], dtype=jnp.int32)
    yp, ys = x_prompt, x_sample
    prompt_rows, sample_rows = [], []
    for l in range(DEPTH):
        lw = (ln1_g[l], w_in[l], q_norm_g[l], k_norm_g[l], w_branch_moba[l], w_branch_sb[l],
              w_out[l], ln2_g[l], w_grp_router[l], b_grp_router[l], w_exp_router[l],
              b_exp_router[l], w_gate[l], w_up[l], w_down[l])
        yp, rows_p = decoder_layer(yp, pos_p, None, *lw)
        past = (gather_pages(cache_k_moba[l], page_table), gather_pages(cache_v_moba[l], page_table),
                gather_pages(cache_k_sb[l], page_table), gather_pages(cache_v_sb[l], page_table))
        ys, rows_s = decoder_layer(ys, pos_s, past, *lw)
        prompt_rows.append(rows_p)
        sample_rows.append(rows_s)
    return (yp, ys,
            _stack(prompt_rows, 0), _stack(prompt_rows, 1), _stack(prompt_rows, 2), _stack(prompt_rows, 3),
            _stack(sample_rows, 0), _stack(sample_rows, 1), _stack(sample_rows, 2), _stack(sample_rows, 3))
```

```python
import functools

import jax
import jax.numpy as jnp
import numpy as np
from jax import lax
from jax.experimental import pallas as pl
from jax.experimental.pallas import tpu as pltpu

F32 = jnp.float32
BF16 = jnp.bfloat16

HEAD_DIM = 64
HALF = HEAD_DIM // 2
N_HEADS = 8
W_ATT = N_HEADS * HEAD_DIM
MOBA_BLOCK = 256
MOBA_TOPK = 3
ROPE_THETA = 10000.0
N_GROUPS = 4
EXPERTS_PER_GROUP = 4
N_EXPERTS = N_GROUPS * EXPERTS_PER_GROUP
RMS_EPS = 1e-6
ATT_SCALE = HEAD_DIM ** -0.5

LANES = 128
HEADS_PER_LANE_BLOCK = LANES // HEAD_DIM
NEG_BIG = -1e30
VMEM_LIMIT = 56 * 1024 * 1024

GATE_PAGES = 16
SB_PAGES = 8


def _cparams(n_axes):
    return pltpu.CompilerParams(dimension_semantics=("arbitrary",) * n_axes,
                                vmem_limit_bytes=VMEM_LIMIT)


def _dot(a, b):
    return lax.dot_general(a, b, (((1,), (0,)), ((), ())), preferred_element_type=F32)


def _dot_nt(a, b):
    return lax.dot_general(a, b, (((1,), (1,)), ((), ())), preferred_element_type=F32)


def _split(x):
    hi = x.astype(BF16)
    lo = (x - hi.astype(F32)).astype(BF16)
    return hi, lo


def _dot3(a, b):
    ah, al = _split(a)
    bh, bl = _split(b)
    return _dot(ah, bh) + (_dot(ah, bl) + _dot(al, bh))


def _dot2_exact_rhs(a, b_bf16):
    ah, al = _split(a)
    return _dot(ah, b_bf16) + _dot(al, b_bf16)


def _const_spec(a, n_grid):
    return pl.BlockSpec(a.shape, lambda *_: (0,) * a.ndim, pipeline_mode=pl.Buffered(1))


def _rms_rows(x, gain):
    ms = jnp.mean(x * x, axis=-1, keepdims=True)
    return x * lax.rsqrt(ms + RMS_EPS) * gain


def _head_norm_rope(p, gain, cos, sin_signed, head_mean):
    ms = _dot2_exact_rhs(p * p, head_mean)
    y = p * lax.rsqrt(ms + RMS_EPS) * gain
    width = p.shape[-1]
    lane = lax.broadcasted_iota(jnp.int32, p.shape, 1)
    first_half = (lane & (HEAD_DIM - 1)) < HALF
    partner = jnp.where(first_half, pltpu.roll(y, width - HALF, 1), pltpu.roll(y, HALF, 1))
    return y * cos + partner * sin_signed


def _head_norm_rope_t(p, gain, cos, sin):
    cols = p.shape[-1]
    p3 = p.reshape(N_HEADS, HEAD_DIM, cols)
    ms = jnp.mean(p3 * p3, axis=1, keepdims=True)
    y = p3 * lax.rsqrt(ms + RMS_EPS) * gain.reshape(N_HEADS, HEAD_DIM, 1)
    x1 = y[:, :HALF]
    x2 = y[:, HALF:]
    out = jnp.concatenate([x1 * cos - x2 * sin, x2 * cos + x1 * sin], axis=1)
    return out.reshape(W_ATT, cols)


def _inproj_prompt_kernel(x_ref, g1_ref, wr_ref, wt_ref, qg_ref, kgt_ref, cos_ref, sin_ref, cost_ref, sint_ref,
                          hm_ref, ka_ref, va_ref, kb_ref, vb_ref, g_ref, qa_ref, qb_ref,
                          kab_ref, vab_ref, kbb_ref, vbb_ref, km_ref):
    it = pl.program_id(1)
    h = _rms_rows(x_ref[...], g1_ref[...]).astype(BF16)

    qa = _dot(h, wr_ref[:, :W_ATT])
    qa_ref[...] = _head_norm_rope(qa, qg_ref[...], cos_ref[...], sin_ref[...], hm_ref[...])
    qb_ref[...] = (_dot(h, wr_ref[:, W_ATT:2 * W_ATT]) * ATT_SCALE).astype(BF16)
    g_ref[...] = _dot(h, wr_ref[:, 2 * W_ATT:])

    def seg_t(k):
        return _dot_nt(wt_ref[k * W_ATT:(k + 1) * W_ATT, :], h)

    ka = _head_norm_rope_t(seg_t(0), kgt_ref[...], cost_ref[...], sint_ref[...])
    for k, full_ref, half_ref in ((0, ka_ref, kab_ref), (1, va_ref, vab_ref), (2, kb_ref, kbb_ref),
                                  (3, vb_ref, vbb_ref)):
        val = ka if k == 0 else seg_t(k)
        full_ref[0] = val
        half_ref[0, 0] = val.astype(BF16)

    @pl.when(it == 0)
    def _():
        km_ref[...] = jnp.zeros(km_ref.shape, F32)

    lane = lax.broadcasted_iota(jnp.int32, km_ref.shape[1:], 1)
    mean = jnp.sum(ka, axis=-1, keepdims=True) * (1.0 / MOBA_BLOCK)
    km_ref[0] = jnp.where(lane == it, mean, km_ref[0])


def _inproj_prompt(x2d, ln1_g, w_rows, w_t, qg, kg_t, cos, sin_signed, cos_t, sin_t, head_mean, *, batch, seq):
    n, d = x2d.shape
    tm = MOBA_BLOCK
    nq = seq // tm
    assert nq <= LANES
    tok = lambda w: pl.BlockSpec((tm, w), lambda b, i: (b * nq + i, 0))
    const = lambda a: _const_spec(a, 2)
    feat = pl.BlockSpec((1, W_ATT, tm), lambda b, i: (b, 0, i))
    feat_blk = pl.BlockSpec((1, 1, W_ATT, tm), lambda b, i: (b, i, 0, 0))
    d_gate = w_rows.shape[1] - 2 * W_ATT
    return pl.pallas_call(
        _inproj_prompt_kernel,
        grid=(batch, nq),
        in_specs=[tok(d), const(ln1_g), const(w_rows), const(w_t), const(qg), const(kg_t),
                  pl.BlockSpec((tm, W_ATT), lambda b, i: (i, 0)), pl.BlockSpec((tm, W_ATT), lambda b, i: (i, 0)),
                  pl.BlockSpec((HALF, tm), lambda b, i: (0, i)), pl.BlockSpec((HALF, tm), lambda b, i: (0, i)),
                  const(head_mean)],
        out_specs=[feat] * 4 + [tok(d_gate), tok(W_ATT), tok(W_ATT)] + [feat_blk] * 4
        + [pl.BlockSpec((1, W_ATT, LANES), lambda b, i: (b, 0, 0))],
        out_shape=[jax.ShapeDtypeStruct((batch, W_ATT, seq), F32)] * 4
        + [jax.ShapeDtypeStruct((n, d_gate), F32), jax.ShapeDtypeStruct((n, W_ATT), F32),
           jax.ShapeDtypeStruct((n, W_ATT), BF16)]
        + [jax.ShapeDtypeStruct((batch, nq, W_ATT, tm), BF16)] * 4
        + [jax.ShapeDtypeStruct((batch, W_ATT, LANES), F32)],
        compiler_params=_cparams(2),
        name="inproj_prompt",
    )(x2d, ln1_g, w_rows, w_t, qg, kg_t, cos, sin_signed, cos_t, sin_t, head_mean)


def _inproj_decode_kernel(x_ref, g1_ref, w_ref, qg_ref, kg_ref, cos_ref, sin_ref, hm_ref,
                          ka_ref, va_ref, kb_ref, vb_ref, g_ref, qa_ref, qb_ref):
    h = _rms_rows(x_ref[...], g1_ref[...]).astype(BF16)

    def seg(k):
        return _dot(h, w_ref[:, k * W_ATT:(k + 1) * W_ATT])

    cos = cos_ref[...]
    sin = sin_ref[...]
    hm = hm_ref[...]
    qa_ref[...] = _head_norm_rope(seg(0), qg_ref[...], cos, sin, hm)
    ka_ref[...] = _head_norm_rope(seg(1), kg_ref[...], cos, sin, hm)
    va_ref[...] = seg(2)
    qb_ref[...] = seg(3)
    kb_ref[...] = seg(4)
    vb_ref[...] = seg(5)
    g_ref[...] = _dot(h, w_ref[:, 6 * W_ATT:])


def _inproj_decode(x2d, ln1_g, w_in_bf16, qg, kg, cos, sin_signed, head_mean):
    n, d = x2d.shape
    d_gate = w_in_bf16.shape[1] - 6 * W_ATT
    full = lambda a: pl.BlockSpec(a.shape, lambda i: (0,) * a.ndim)
    args = (x2d, ln1_g, w_in_bf16, qg, kg, cos, sin_signed, head_mean)
    out_shape = ([jax.ShapeDtypeStruct((n, W_ATT), F32)] * 4 + [jax.ShapeDtypeStruct((n, d_gate), F32)]
                 + [jax.ShapeDtypeStruct((n, W_ATT), F32)] * 2)
    return pl.pallas_call(
        _inproj_decode_kernel,
        grid=(1,),
        in_specs=[full(a) for a in args],
        out_specs=[full(s) for s in out_shape],
        out_shape=out_shape,
        compiler_params=_cparams(1),
        name="inproj_decode",
    )(*args)


def _topk_lowest_index(gate, n_valid, n_pick):
    col = lax.broadcasted_iota(jnp.int32, gate.shape, 1)
    g = jnp.where(col < n_valid, gate, -jnp.inf)
    picks = []
    for t in range(n_pick):
        m = jnp.max(g, axis=-1, keepdims=True)
        idx = jnp.min(jnp.where(g == m, col, gate.shape[1]), axis=-1, keepdims=True)
        picks.append((idx, t < n_valid))
        g = jnp.where(col == idx, -jnp.inf, g)
    return picks


def _moba_prompt_kernel(q_ref, k_ref, v_ref, km_ref, o_ref, m_scr, l_scr, acc_scr):
    own = pl.program_id(2)
    tq = q_ref.shape[0]
    row = lax.broadcasted_iota(jnp.int32, (tq, MOBA_BLOCK), 0)
    colk = lax.broadcasted_iota(jnp.int32, (tq, MOBA_BLOCK), 1)
    blk_row = lax.broadcasted_iota(jnp.int32, (LANES, MOBA_BLOCK), 0)
    blk_col = lax.broadcasted_iota(jnp.int32, (tq, LANES), 1)
    q_all = q_ref[...]
    outs = []
    for hh in range(HEADS_PER_LANE_BLOCK):
        rows = slice(hh * HEAD_DIM, (hh + 1) * HEAD_DIM)
        q = q_all[:, rows]
        qb = (q * ATT_SCALE).astype(BF16)
        gate = _dot3(q, km_ref[0, rows, :])
        sel = jnp.zeros(gate.shape, jnp.bool_)
        for idx, ok in _topk_lowest_index(gate, own, MOBA_TOPK):
            sel = sel | ((blk_col == idx) & ok)
        negsel = jnp.where(sel, 0.0, NEG_BIG).astype(BF16)

        s = _dot(qb, k_ref[0, own, rows, :])
        s = jnp.where(colk <= row, s, NEG_BIG)
        m0 = jnp.max(s, axis=-1, keepdims=True)
        p = jnp.exp(s - m0)
        m_scr[...] = m0
        l_scr[...] = jnp.sum(p, axis=-1, keepdims=True)
        acc_scr[...] = _dot_nt(p.astype(BF16), v_ref[0, own, rows, :])

        def body(j, carry):
            onehot = jnp.where(blk_row == j, 1.0, 0.0).astype(BF16)
            sj = _dot(qb, k_ref[0, j, rows, :]) + _dot(negsel, onehot)
            m_old = m_scr[...]
            m_new = jnp.maximum(m_old, jnp.max(sj, axis=-1, keepdims=True))
            alpha = jnp.exp(m_old - m_new)
            pj = jnp.exp(sj - m_new)
            l_scr[...] = alpha * l_scr[...] + jnp.sum(pj, axis=-1, keepdims=True)
            acc_scr[...] = alpha * acc_scr[...] + _dot_nt(pj.astype(BF16), v_ref[0, j, rows, :])
            m_scr[...] = m_new
            return carry

        lax.fori_loop(0, own, body, 0)
        outs.append(acc_scr[...] / l_scr[...])
    o_ref[...] = jnp.concatenate(outs, axis=-1).astype(o_ref.dtype)


def _kv_block_specs(nq):
    spec = pl.BlockSpec((1, nq, LANES, MOBA_BLOCK), lambda b, g, i: (b, 0, g, 0))
    return [spec, spec]


def _moba_prompt(qa, ka_blk, va_blk, kmean_t, *, batch, seq):
    n = batch * seq
    nq = seq // MOBA_BLOCK
    return pl.pallas_call(
        _moba_prompt_kernel,
        grid=(batch, N_HEADS // HEADS_PER_LANE_BLOCK, nq),
        in_specs=[pl.BlockSpec((MOBA_BLOCK, LANES), lambda b, g, i: (b * nq + i, g))] + _kv_block_specs(nq)
        + [pl.BlockSpec((1, LANES, LANES), lambda b, g, i: (b, g, 0))],
        out_specs=pl.BlockSpec((MOBA_BLOCK, LANES), lambda b, g, i: (b * nq + i, g)),
        out_shape=jax.ShapeDtypeStruct((n, W_ATT), BF16),
        scratch_shapes=[pltpu.VMEM((MOBA_BLOCK, 1), F32), pltpu.VMEM((MOBA_BLOCK, 1), F32),
                        pltpu.VMEM((MOBA_BLOCK, HEAD_DIM), F32)],
        compiler_params=_cparams(3),
        name="moba_prompt",
    )(qa, ka_blk, va_blk, kmean_t)


def _sb_tile(q_bf16, kt_bf16, vt_bf16, suffix_ones, carry, mask):
    z = _dot(q_bf16, kt_bf16)
    lp = jnp.log(1.0 + jnp.exp(-jnp.abs(z)))
    l1_raw = -(jnp.maximum(z, 0.0) + lp)
    l1 = l1_raw if mask is None else jnp.where(mask, l1_raw, 0.0)
    hi, lo = _split(l1)
    local = _dot(hi, suffix_ones) + _dot(lo, suffix_ones)
    a = jnp.exp((l1_raw + z) + (local + carry))
    if mask is not None:
        a = jnp.where(mask, a, 0.0)
    new_carry = carry + (local[:, 0:1] + l1[:, 0:1])
    return _dot_nt(a.astype(BF16), vt_bf16), new_carry


def _sb_prompt_kernel(q_ref, k_ref, v_ref, u_ref, o_ref, c_scr, acc_scr):
    own = pl.program_id(2)
    tq = q_ref.shape[0]
    row = lax.broadcasted_iota(jnp.int32, (tq, tq), 0)
    col = lax.broadcasted_iota(jnp.int32, (tq, tq), 1)
    u = u_ref[...]
    q_all = q_ref[...]
    outs = []
    for hh in range(HEADS_PER_LANE_BLOCK):
        rows = slice(hh * HEAD_DIM, (hh + 1) * HEAD_DIM)
        q = q_all[:, rows]
        o0, c0 = _sb_tile(q, k_ref[0, own, rows, :], v_ref[0, own, rows, :], u,
                          jnp.zeros((tq, 1), F32), col < row)
        acc_scr[...] = o0
        c_scr[...] = c0

        def body(t, carry):
            j = own - 1 - t
            oj, cj = _sb_tile(q, k_ref[0, j, rows, :], v_ref[0, j, rows, :], u, c_scr[...], None)
            acc_scr[...] += oj
            c_scr[...] = cj
            return carry

        lax.fori_loop(0, own, body, 0)
        outs.append(acc_scr[...])
    o_ref[...] = jnp.concatenate(outs, axis=-1).astype(o_ref.dtype)


def _sb_prompt(qb, kb_blk, vb_blk, suffix_ones, *, batch, seq):
    n = batch * seq
    tq = MOBA_BLOCK
    nq = seq // tq
    return pl.pallas_call(
        _sb_prompt_kernel,
        grid=(batch, N_HEADS // HEADS_PER_LANE_BLOCK, nq),
        in_specs=[pl.BlockSpec((tq, LANES), lambda b, g, i: (b * nq + i, g))] + _kv_block_specs(nq)
        + [pl.BlockSpec(suffix_ones.shape, lambda b, g, i: (0, 0))],
        out_specs=pl.BlockSpec((tq, LANES), lambda b, g, i: (b * nq + i, g)),
        out_shape=jax.ShapeDtypeStruct((n, W_ATT), BF16),
        scratch_shapes=[pltpu.VMEM((tq, 1), F32), pltpu.VMEM((tq, HEAD_DIM), F32)],
        compiler_params=_cparams(3),
        name="sb_prompt",
    )(qb, kb_blk, vb_blk, suffix_ones)


def _head_of_feature(f):
    return f >> (HEAD_DIM.bit_length() - 1)


def _query_head(row, dec_t):
    return row >> (dec_t.bit_length() - 1)


def _block_diag_rows(qrep, dec_t):
    row = lax.broadcasted_iota(jnp.int32, qrep.shape, 0)
    col = lax.broadcasted_iota(jnp.int32, qrep.shape, 1)
    return jnp.where(_head_of_feature(col) == _query_head(row, dec_t), qrep, 0.0)


def _sb_decode_kernel(pt_ref, q_ref, knew_ref, vnew_ref, u_ref, *refs, dec_t, n_chunks):
    k_pages = refs[:SB_PAGES]
    v_pages = refs[SB_PAGES:2 * SB_PAGES]
    o_ref, c_scr, acc_scr = refs[2 * SB_PAGES:]
    c = pl.program_id(1)
    rows = q_ref.shape[1]
    tk = u_ref.shape[0]
    pages_per_tile = tk // k_pages[0].shape[2]
    u = u_ref[...]
    q = (_block_diag_rows(q_ref[0], dec_t) * ATT_SCALE).astype(BF16)

    @pl.when(c == 0)
    def _():
        r = lax.broadcasted_iota(jnp.int32, (rows, tk), 0)
        kk = lax.broadcasted_iota(jnp.int32, (rows, tk), 1)
        mask = kk < (r & (dec_t - 1))
        o0, c0 = _sb_tile(q, knew_ref[0], vnew_ref[0], u, jnp.zeros((rows, 1), F32), mask)
        acc_scr[...] = o0
        c_scr[...] = c0

    for t in reversed(range(SB_PAGES // pages_per_tile)):
        kt = jnp.concatenate([k_pages[t * pages_per_tile + r][0] for r in range(pages_per_tile)], axis=1)
        vt = jnp.concatenate([v_pages[t * pages_per_tile + r][0] for r in range(pages_per_tile)], axis=1)
        ot, ct = _sb_tile(q, kt.astype(BF16), vt.astype(BF16), u, c_scr[...], None)
        acc_scr[...] += ot
        c_scr[...] = ct

    @pl.when(c == n_chunks - 1)
    def _():
        acc = acc_scr[...]
        row = lax.broadcasted_iota(jnp.int32, acc.shape, 0)
        col = lax.broadcasted_iota(jnp.int32, acc.shape, 1)
        diag = jnp.where(_head_of_feature(col) == _query_head(row, dec_t), acc, 0.0)
        o_ref[0] = jnp.sum(diag.reshape(N_HEADS, dec_t, acc.shape[1]), axis=0).astype(o_ref.dtype)


def _sb_decode(page_table_flat, qrep, knew_t, vnew_t, suffix_ones, cache_kt, cache_vt, *, dec_b, dec_t, n_pages):
    n_chunks = n_pages // SB_PAGES
    width, page = cache_kt.shape[1], cache_kt.shape[2]
    tk = suffix_ones.shape[0]

    def page_spec(r):
        return pl.BlockSpec(
            (1, width, page),
            lambda b, c, pt: (pt[b * n_pages + n_pages - SB_PAGES * (c + 1) + r], 0, 0))

    grid_spec = pltpu.PrefetchScalarGridSpec(
        num_scalar_prefetch=1,
        grid=(dec_b, n_chunks),
        in_specs=[
            pl.BlockSpec((1, N_HEADS * dec_t, width), lambda b, c, pt: (b, 0, 0)),
            pl.BlockSpec((1, width, tk), lambda b, c, pt: (b, 0, 0)),
            pl.BlockSpec((1, width, tk), lambda b, c, pt: (b, 0, 0)),
            pl.BlockSpec(suffix_ones.shape, lambda b, c, pt: (0, 0)),
        ] + [page_spec(r) for r in range(SB_PAGES)] * 2,
        out_specs=pl.BlockSpec((1, dec_t, width), lambda b, c, pt: (b, 0, 0)),
        scratch_shapes=[pltpu.VMEM((N_HEADS * dec_t, 1), F32), pltpu.VMEM((N_HEADS * dec_t, width), F32)],
    )
    return pl.pallas_call(
        functools.partial(_sb_decode_kernel, dec_t=dec_t, n_chunks=n_chunks),
        grid_spec=grid_spec,
        out_shape=jax.ShapeDtypeStruct((dec_b, dec_t, width), BF16),
        compiler_params=_cparams(2),
        name="sb_decode",
    )(page_table_flat, qrep, knew_t, vnew_t, suffix_ones, *([cache_kt] * SB_PAGES), *([cache_vt] * SB_PAGES))


def _moba_gate_kernel(pt_ref, q_ref, *refs, dec_t, n_chunks, n_blocks):
    k_pages = refs[:GATE_PAGES]
    sel_ref, km_scr = refs[GATE_PAGES:]
    c = pl.program_id(1)
    pages_per_block = MOBA_BLOCK // k_pages[0].shape[2]
    blocks = GATE_PAGES // pages_per_block
    @pl.when(c == 0)
    def _():
        km_scr[...] = jnp.zeros(km_scr.shape, F32)

    lane = lax.broadcasted_iota(jnp.int32, km_scr.shape, 1)
    km = km_scr[...]
    for blk in range(blocks):
        s = k_pages[blk * pages_per_block][0]
        for r in range(1, pages_per_block):
            s = s + k_pages[blk * pages_per_block + r][0]
        mean = jnp.sum(s, axis=-1, keepdims=True) * (1.0 / MOBA_BLOCK)
        km = jnp.where(lane == c * blocks + blk, mean, km)
    km_scr[...] = km

    @pl.when(c == n_chunks - 1)
    def _():
        gate = _dot3(_block_diag_rows(q_ref[0], dec_t), km)
        lane_o = lax.broadcasted_iota(jnp.int32, gate.shape, 1)
        out = jnp.zeros(gate.shape, jnp.int32)
        for t, (idx, _) in enumerate(_topk_lowest_index(gate, n_blocks, MOBA_TOPK)):
            out = jnp.where(lane_o == t, idx, out)
        sel_ref[0] = out


def _moba_gate(page_table_flat, qrep, cache_kt, *, dec_b, dec_t, n_pages):
    n_chunks = n_pages // GATE_PAGES
    width, page = cache_kt.shape[1], cache_kt.shape[2]
    n_blocks = n_pages * page // MOBA_BLOCK
    assert MOBA_TOPK <= n_blocks <= LANES
    rows = N_HEADS * dec_t

    def page_spec(r):
        return pl.BlockSpec((1, width, page), lambda b, c, pt: (pt[b * n_pages + GATE_PAGES * c + r], 0, 0))

    grid_spec = pltpu.PrefetchScalarGridSpec(
        num_scalar_prefetch=1,
        grid=(dec_b, n_chunks),
        in_specs=[pl.BlockSpec((1, rows, width), lambda b, c, pt: (b, 0, 0))]
        + [page_spec(r) for r in range(GATE_PAGES)],
        out_specs=pl.BlockSpec((1, rows, LANES), lambda b, c, pt: (b, 0, 0)),
        scratch_shapes=[pltpu.VMEM((width, LANES), F32)],
    )
    return pl.pallas_call(
        functools.partial(_moba_gate_kernel, dec_t=dec_t, n_chunks=n_chunks, n_blocks=n_blocks),
        grid_spec=grid_spec,
        out_shape=jax.ShapeDtypeStruct((dec_b, rows, LANES), jnp.int32),
        compiler_params=_cparams(2),
        name="moba_gate",
    )(page_table_flat, qrep, *([cache_kt] * GATE_PAGES))


def _moba_decode_kernel(pt_ref, sel_ref, q_ref, knew_ref, vnew_ref, *refs, dec_t, n_slots, pages_per_block):
    n_pg = n_slots * pages_per_block
    k_pages = refs[:n_pg]
    v_pages = refs[n_pg:2 * n_pg]
    o_ref = refs[2 * n_pg]
    qrows = q_ref.shape[2]
    q = (q_ref[0, 0] * ATT_SCALE).astype(BF16)

    row = lax.broadcasted_iota(jnp.int32, (qrows, MOBA_BLOCK), 0)
    kk = lax.broadcasted_iota(jnp.int32, (qrows, MOBA_BLOCK), 1)
    scores = []
    values = []
    for slot in range(n_slots):
        kt = jnp.concatenate([k_pages[slot * pages_per_block + r][0] for r in range(pages_per_block)], axis=1)
        vt = jnp.concatenate([v_pages[slot * pages_per_block + r][0] for r in range(pages_per_block)], axis=1)
        s = _dot(q, kt.astype(BF16))
        scores.append(jnp.where(row == slot // MOBA_TOPK, s, NEG_BIG))
        values.append(vt.astype(BF16))
    s_new = _dot(q, knew_ref[0])
    scores.append(jnp.where((kk <= row) & (kk < dec_t), s_new, NEG_BIG))
    values.append(vnew_ref[0])

    m = scores[0].max(axis=-1, keepdims=True)
    for s in scores[1:]:
        m = jnp.maximum(m, s.max(axis=-1, keepdims=True))
    l = jnp.zeros((qrows, 1), F32)
    acc = jnp.zeros((qrows, HEAD_DIM), F32)
    for s, v in zip(scores, values):
        p = jnp.exp(s - m)
        l = l + jnp.sum(p, axis=-1, keepdims=True)
        acc = acc + _dot_nt(p.astype(BF16), v)
    o_ref[0, 0] = acc / l


def _moba_decode(page_table_flat, sel_flat, q_heads, knew_t, vnew_t, cache_kt, cache_vt, *, dec_b, dec_t, n_pages):
    page = cache_kt.shape[2]
    pages_per_block = MOBA_BLOCK // page
    n_slots = dec_t * MOBA_TOPK
    qrows = q_heads.shape[2]

    def page_spec(slot, r):
        def index_map(b, h, pt, sel):
            blk = sel[(b * N_HEADS + h) * n_slots + slot]
            return (pt[b * n_pages + blk * pages_per_block + r], h, 0)
        return pl.BlockSpec((1, HEAD_DIM, page), index_map)

    pages = [page_spec(s, r) for s in range(n_slots) for r in range(pages_per_block)]
    grid_spec = pltpu.PrefetchScalarGridSpec(
        num_scalar_prefetch=2,
        grid=(dec_b, N_HEADS),
        in_specs=[
            pl.BlockSpec((1, 1, qrows, HEAD_DIM), lambda b, h, pt, sel: (b, h, 0, 0)),
            pl.BlockSpec((1, HEAD_DIM, MOBA_BLOCK), lambda b, h, pt, sel: (b, h, 0)),
            pl.BlockSpec((1, HEAD_DIM, MOBA_BLOCK), lambda b, h, pt, sel: (b, h, 0)),
        ] + pages * 2,
        out_specs=pl.BlockSpec((1, 1, qrows, HEAD_DIM), lambda b, h, pt, sel: (b, h, 0, 0)),
    )
    return pl.pallas_call(
        functools.partial(_moba_decode_kernel, dec_t=dec_t, n_slots=n_slots, pages_per_block=pages_per_block),
        grid_spec=grid_spec,
        out_shape=jax.ShapeDtypeStruct((dec_b, N_HEADS, qrows, HEAD_DIM), F32),
        compiler_params=_cparams(2),
        name="moba_decode",
    )(page_table_flat, sel_flat, q_heads, knew_t, vnew_t,
      *([cache_kt] * len(pages)), *([cache_vt] * len(pages)))


def _route(logits):
    lane = lax.broadcasted_iota(jnp.int32, logits.shape, 1)

    def softmax(mask):
        x = jnp.where(mask, logits, -jnp.inf)
        e = jnp.exp(x - jnp.max(x, axis=-1, keepdims=True))
        return e / jnp.sum(e, axis=-1, keepdims=True)

    def top1(p, mask):
        best = jnp.max(jnp.where(mask, p, -1.0), axis=-1, keepdims=True)
        idx = jnp.min(jnp.where(mask & (p == best), lane, LANES), axis=-1, keepdims=True)
        return best, idx

    grp_mask = (lane >= N_EXPERTS) & (lane < N_EXPERTS + N_GROUPS)
    grp_p, grp_lane = top1(softmax(grp_mask), grp_mask)
    first = (grp_lane - N_EXPERTS) * EXPERTS_PER_GROUP
    in_mask = (lane >= first) & (lane < first + EXPERTS_PER_GROUP)
    in_p = softmax(in_mask)
    p1, i1 = top1(in_p, in_mask)
    rest = in_mask & (lane != i1)
    p2, i2 = top1(in_p, rest)
    total = p1 + p2
    w1 = grp_p * (p1 / total)
    w2 = grp_p * (p2 / total)
    return jnp.where(lane == i1, w1, jnp.where(lane == i2, w2, 0.0))


def _post_kernel(x_ref, oa_ref, ob_ref, g_ref, wa_ref, wb_ref, wo_ref, g2_ref, wr_ref, br_ref,
                 x1_ref, h2_ref, gates_ref):
    d = x_ref.shape[1]
    ya = _dot(oa_ref[...], wa_ref[...])
    yb = _dot(ob_ref[...], wb_ref[...])
    g = g_ref[...]
    merged = jax.nn.sigmoid(g[:, :d]) * ya + jax.nn.sigmoid(g[:, d:]) * yb
    x1 = x_ref[...] + _dot(merged.astype(BF16), wo_ref[...])
    x1_ref[...] = x1
    h2 = _rms_rows(x1, g2_ref[...])
    h2_ref[...] = h2.astype(BF16)
    gates_ref[...] = _route(_dot3(h2, wr_ref[...]) + br_ref[...])


def _post(x2d, oa, ob, g, wa, wb, wo, ln2_g, w_router, b_router, *, tm):
    n, d = x2d.shape
    tok = lambda w: pl.BlockSpec((tm, w), lambda i: (i, 0))
    const = lambda a: _const_spec(a, 1)
    return pl.pallas_call(
        _post_kernel,
        grid=(n // tm,),
        in_specs=[tok(d), tok(W_ATT), tok(W_ATT), tok(2 * d), const(wa), const(wb), const(wo), const(ln2_g),
                  const(w_router), const(b_router)],
        out_specs=[tok(d), tok(d), tok(LANES)],
        out_shape=[jax.ShapeDtypeStruct((n, d), F32), jax.ShapeDtypeStruct((n, d), BF16),
                   jax.ShapeDtypeStruct((n, LANES), F32)],
        compiler_params=_cparams(1),
        name="post",
    )(x2d, oa, ob, g, wa, wb, wo, ln2_g, w_router, b_router)


def _moe_kernel(x1_ref, h2_ref, gates_ref, wg_ref, wu_ref, wd_ref, o_ref, acc_scr):
    e = pl.program_id(1)

    @pl.when(e == 0)
    def _():
        acc_scr[...] = x1_ref[...]

    h = h2_ref[...]
    a = _dot(h, wg_ref[0])
    u = _dot(h, wu_ref[0])
    gates = gates_ref[...]
    lane = lax.broadcasted_iota(jnp.int32, gates.shape, 1)
    ge = jnp.sum(jnp.where(lane == e, gates, 0.0), axis=-1, keepdims=True)
    hid = (a * jax.nn.sigmoid(a)) * u * ge
    acc_scr[...] += _dot(hid.astype(BF16), wd_ref[0])

    @pl.when(e == pl.num_programs(1) - 1)
    def _():
        o_ref[...] = acc_scr[...]


def _moe(x1, h2, gates, wg, wu, wd, *, tm):
    n, d = x1.shape
    n_exp, _, f = wg.shape
    return pl.pallas_call(
        _moe_kernel,
        grid=(n // tm, n_exp),
        in_specs=[
            pl.BlockSpec((tm, d), lambda i, e: (i, 0)),
            pl.BlockSpec((tm, d), lambda i, e: (i, 0)),
            pl.BlockSpec((tm, LANES), lambda i, e: (i, 0)),
            pl.BlockSpec((1, d, f), lambda i, e: (e, 0, 0)),
            pl.BlockSpec((1, d, f), lambda i, e: (e, 0, 0)),
            pl.BlockSpec((1, f, d), lambda i, e: (e, 0, 0)),
        ],
        out_specs=pl.BlockSpec((tm, d), lambda i, e: (i, 0)),
        out_shape=jax.ShapeDtypeStruct((n, d), F32),
        scratch_shapes=[pltpu.VMEM((tm, d), F32)],
        compiler_params=_cparams(2),
        name="moe",
    )(x1, h2, gates, wg, wu, wd)


def _rope_tables(pos):
    inv_freq = ROPE_THETA ** (-jnp.arange(HALF, dtype=F32) / HALF)
    ang = pos.astype(F32)[:, None] * inv_freq[None, :]
    return jnp.cos(ang), jnp.sin(ang)


def _rope_tables_rows(cos, sin):
    cos_full = jnp.tile(jnp.concatenate([cos, cos], axis=-1), (1, N_HEADS))
    sin_signed = jnp.tile(jnp.concatenate([-sin, sin], axis=-1), (1, N_HEADS))
    return cos_full, sin_signed


def _suffix_ones(n):
    j = np.arange(n)
    return jnp.asarray(j[:, None] > j[None, :], dtype=BF16)


def _tile_rows(n, want):
    return want if n % want == 0 else n


def _feature_major_cache(cache):
    _, n_phys, page, n_heads, dh = cache.shape
    return cache[0].transpose(0, 2, 3, 1).reshape(n_phys, n_heads * dh, page)


def kernel(x_prompt, x_sample, cache_k_moba, cache_v_moba, cache_k_sb, cache_v_sb, page_table, ln1_g, w_in,
           q_norm_g, k_norm_g, w_branch_moba, w_branch_sb, w_out, ln2_g, w_grp_router, b_grp_router,
           w_exp_router, b_exp_router, w_gate, w_up, w_down):
    batch, seq, d = x_prompt.shape
    dec_b, dec_t, _ = x_sample.shape
    depth, n_phys, page = cache_k_moba.shape[:3]
    n_pages = page_table.shape[1]
    past_len = n_pages * page
    assert depth == 1 and w_in.shape[0] == 1
    assert seq % MOBA_BLOCK == 0 and past_len % MOBA_BLOCK == 0 and MOBA_BLOCK % page == 0
    assert dec_t <= 8 and dec_t & (dec_t - 1) == 0
    assert n_pages % GATE_PAGES == 0 and n_pages % SB_PAGES == 0

    w_in_b = w_in[0].astype(BF16)
    seg = lambda k: w_in_b[:, k * W_ATT:(k + 1) * W_ATT]
    w_rows = jnp.concatenate([seg(0), seg(3), w_in_b[:, 6 * W_ATT:]], axis=1)
    w_t = jnp.concatenate([seg(1), seg(2), seg(4), seg(5)], axis=1).T
    wa_b = w_branch_moba[0].astype(BF16)
    wb_b = w_branch_sb[0].astype(BF16)
    wo_b = w_out[0].astype(BF16)
    wg_b = w_gate[0].astype(BF16)
    wu_b = w_up[0].astype(BF16)
    wd_b = w_down[0].astype(BF16)
    ln1 = ln1_g[0][None, :]
    ln2 = ln2_g[0][None, :]
    qg = jnp.tile(q_norm_g[0], N_HEADS)[None, :]
    kg = jnp.tile(k_norm_g[0], N_HEADS)[None, :]
    kg_t = jnp.tile(k_norm_g[0], N_HEADS)[:, None]
    w_router = jnp.zeros((d, LANES), F32)
    w_router = w_router.at[:, :N_EXPERTS].set(w_exp_router[0]).at[:, N_EXPERTS:N_EXPERTS + N_GROUPS].set(
        w_grp_router[0])
    b_router = jnp.zeros((1, LANES), F32)
    b_router = b_router.at[0, :N_EXPERTS].set(b_exp_router[0]).at[0, N_EXPERTS:N_EXPERTS + N_GROUPS].set(
        b_grp_router[0])
    lane_head = np.arange(W_ATT) // HEAD_DIM
    head_mean = jnp.asarray((lane_head[:, None] == lane_head[None, :]) / HEAD_DIM, dtype=BF16)
    suffix_ones = _suffix_ones(MOBA_BLOCK)

    n_p = batch * seq
    xp = x_prompt.reshape(n_p, d)
    cos_p, sin_p = _rope_tables(jnp.arange(seq, dtype=jnp.int32))
    cos_rows, sin_rows = _rope_tables_rows(cos_p, sin_p)
    (ka_p, va_p, kb_p, vb_p, g_p, qa_p, qb_p, ka_blk, va_blk, kb_blk, vb_blk, kmean_t) = _inproj_prompt(
        xp, ln1, w_rows, w_t, qg, kg_t, cos_rows, sin_rows, cos_p.T, sin_p.T, head_mean, batch=batch, seq=seq)
    oa_p = _moba_prompt(qa_p, ka_blk, va_blk, kmean_t, batch=batch, seq=seq)
    ob_p = _sb_prompt(qb_p, kb_blk, vb_blk, suffix_ones, batch=batch, seq=seq)
    x1_p, h2_p, gates_p = _post(xp, oa_p, ob_p, g_p, wa_b, wb_b, wo_b, ln2, w_router, b_router,
                                tm=_tile_rows(n_p, 256))
    y_p = _moe(x1_p, h2_p, gates_p, wg_b, wu_b, wd_b, tm=_tile_rows(n_p, 512))

    n_s = dec_b * dec_t
    xs = x_sample.reshape(n_s, d)
    cos_s, sin_s = _rope_tables_rows(*_rope_tables(past_len + jnp.arange(dec_t, dtype=jnp.int32)))
    (ka_s, va_s, kb_s, vb_s, g_s, qa_s, qb_s) = _inproj_decode(
        xs, ln1, w_in_b, qg, kg, jnp.tile(cos_s, (dec_b, 1)), jnp.tile(sin_s, (dec_b, 1)), head_mean)

    pt_flat = page_table.reshape(-1).astype(jnp.int32)
    ckm, cvm, cks, cvs = (_feature_major_cache(c) for c in (cache_k_moba, cache_v_moba, cache_k_sb, cache_v_sb))

    def new_tile(a):
        a = a.reshape(dec_b, dec_t, W_ATT).transpose(0, 2, 1).astype(BF16)
        return jnp.pad(a, ((0, 0), (0, 0), (0, MOBA_BLOCK - dec_t)))

    def rep_heads(q):
        return jnp.tile(q.reshape(dec_b, dec_t, W_ATT), (1, N_HEADS, 1))

    sel = _moba_gate(pt_flat, rep_heads(qa_s), ckm, dec_b=dec_b, dec_t=dec_t, n_pages=n_pages)
    sel_flat = sel[:, :, :MOBA_TOPK].reshape(-1)
    q_heads = qa_s.reshape(dec_b, dec_t, N_HEADS, HEAD_DIM).transpose(0, 2, 1, 3)
    q_heads = jnp.pad(q_heads, ((0, 0), (0, 0), (0, 8 - dec_t), (0, 0)))
    oa_s = _moba_decode(pt_flat, sel_flat, q_heads, new_tile(ka_s), new_tile(va_s), ckm, cvm,
                        dec_b=dec_b, dec_t=dec_t, n_pages=n_pages)
    oa_s = oa_s[:, :, :dec_t].transpose(0, 2, 1, 3).reshape(n_s, W_ATT).astype(BF16)
    ob_s = _sb_decode(pt_flat, rep_heads(qb_s), new_tile(kb_s), new_tile(vb_s), suffix_ones, cks, cvs,
                      dec_b=dec_b, dec_t=dec_t, n_pages=n_pages)
    x1_s, h2_s, gates_s = _post(xs, oa_s, ob_s.reshape(n_s, W_ATT), g_s, wa_b, wb_b, wo_b, ln2, w_router,
                                b_router, tm=n_s)
    y_s = _moe(x1_s, h2_s, gates_s, wg_b, wu_b, wd_b, tm=n_s)

    def heads_t(a):
        return a.reshape(batch, N_HEADS, HEAD_DIM, seq).transpose(0, 3, 1, 2)[None]

    def heads(a):
        return a.reshape(1, dec_b, dec_t, N_HEADS, HEAD_DIM)

    return (y_p.reshape(batch, seq, d), y_s.reshape(dec_b, dec_t, d),
            heads_t(ka_p), heads_t(va_p), heads_t(kb_p), heads_t(vb_p),
            heads(ka_s), heads(va_s), heads(kb_s), heads(vb_s))
```
